```python
import math
import jax, jax.numpy as jnp
from jax import lax
import numpy as np

D_MODEL = 1024
BATCH = 4
SEQ = 8192
DEPTH = 1

N_META = 16
CHUNK = 64
PAD = CHUNK - N_META
M_HEADS = 4
R_HEADS = 4
HEAD_DIM = D_MODEL // (M_HEADS + R_HEADS)
M_WIDTH = M_HEADS * HEAD_DIM
R_WIDTH = R_HEADS * HEAD_DIM
MIX_WIDTH = M_WIDTH + R_WIDTH
PROJ_WIDTH = 4 * M_WIDTH + 4 * R_WIDTH + 2 * M_HEADS
PEER_HEADS = 8
PEER_KEYS = 128
PEER_EXPERTS = PEER_KEYS * PEER_KEYS
PEER_TOPK = 16
PEER_QDIM = 256
PEER_TOKEN_BLOCK = 256
ROPE_BASE = 10000.0
EPS = 1e-6

kernel_name = "hymba_mlstm_retnet_peer_block"

F32 = jnp.float32


def rmsnorm(x, w):
    xf = x.astype(F32)
    y = xf * lax.rsqrt(jnp.mean(xf * xf, axis=-1, keepdims=True) + EPS) * w.astype(F32)
    return y.astype(x.dtype)


def head_norm(x, w):
    H, d = x.shape[1], x.shape[3]
    mu = jnp.mean(x, axis=-1, keepdims=True)
    xc = x - mu
    var = jnp.mean(xc * xc, axis=-1, keepdims=True)
    return xc * lax.rsqrt(var + EPS) * w.astype(F32).reshape(1, H, 1, d)


def split_heads(t, n_heads):
    B, L, _ = t.shape
    return t.reshape(B, L, n_heads, -1).transpose(0, 2, 1, 3)


def merge_heads(t):
    B, H, L, d = t.shape
    return t.transpose(0, 2, 1, 3).reshape(B, L, H * d)


def to_chunks(t):
    B, H, L, d = t.shape
    return t.reshape(B, H, L // CHUNK, CHUNK, d)


def rotary(x, pos):
    half = x.shape[-1] // 2
    inv_freq = ROPE_BASE ** (-jnp.arange(half, dtype=F32) / half)
    ang = pos[:, None] * inv_freq[None, :]
    cos, sin = jnp.cos(ang), jnp.sin(ang)
    x1, x2 = x[..., :half], x[..., half:]
    return jnp.concatenate([x1 * cos - x2 * sin, x1 * sin + x2 * cos], axis=-1)


def mlstm_heads(q, k, v, i_pre, f_pre):
    B, H, N, C, dk = k.shape
    dv = v.shape[-1]
    k = k * (dk ** -0.5)
    log_f = jax.nn.log_sigmoid(f_pre)
    log_i = i_pre
    b = jnp.cumsum(log_f, axis=-1)
    g = b[..., -1]
    a = g[..., None] - b + log_i

    def step(carry, inp):
        c_st, n_st, m_st = carry
        k_c, v_c, a_c, g_c = inp
        m_new = jnp.maximum(g_c + m_st, jnp.max(a_c, axis=-1))
        decay = jnp.exp(g_c + m_st - m_new)
        w = jnp.exp(a_c - m_new[..., None])
        kw = k_c * w[..., None]
        c_new = decay[..., None, None] * c_st + jnp.einsum('bhjk,bhjv->bhkv', kw, v_c)
        n_new = decay[..., None] * n_st + jnp.sum(kw, axis=2)
        return (c_new, n_new, m_new), (c_st, n_st, m_st)

    init = (jnp.zeros((B, H, dk, dv), F32), jnp.zeros((B, H, dk), F32), jnp.zeros((B, H), F32))
    xs = (jnp.moveaxis(k, 2, 0), jnp.moveaxis(v, 2, 0), jnp.moveaxis(a, 2, 0), jnp.moveaxis(g, 2, 0))
    _, (c_prev, n_prev, m_prev) = lax.scan(step, init, xs)
    c_prev = jnp.moveaxis(c_prev, 0, 2)
    n_prev = jnp.moveaxis(n_prev, 0, 2)
    m_prev = jnp.moveaxis(m_prev, 0, 2)

    causal = jnp.tril(jnp.ones((C, C), dtype=bool))
    d_log = b[..., :, None] - b[..., None, :] + log_i[..., None, :]
    d_log = jnp.where(causal, d_log, -jnp.inf)
    inter_log = b + m_prev[..., None]
    m_t = jnp.maximum(inter_log, jnp.max(d_log, axis=-1))
    w_inter = jnp.exp(inter_log - m_t)
    s = jnp.einsum('bhncd,bhnjd->bhncj', q, k) * jnp.exp(d_log - m_t[..., None])
    num = (w_inter[..., None] * jnp.einsum('bhncd,bhndv->bhncv', q, c_prev)
           + jnp.einsum('bhncj,bhnjv->bhncv', s, v))
    den = w_inter * jnp.einsum('bhncd,bhnd->bhnc', q, n_prev) + jnp.sum(s, axis=-1)
    return num / jnp.maximum(jnp.abs(den), jnp.exp(-m_t))[..., None]


def retention_heads(q, k, v):
    B, H, N, C, dk = k.shape
    dv = v.shape[-1]
    k = k * (dk ** -0.5)
    log_gamma = jnp.log1p(-jnp.exp2(-5.0 - jnp.arange(H, dtype=F32)))
    idx = jnp.arange(C, dtype=F32)
    rel = idx[:, None] - idx[None, :]
    causal = rel >= 0
    decay_intra = jnp.where(causal, jnp.exp(log_gamma[:, None, None] * jnp.where(causal, rel, 0.0)), 0.0)
    decay_cross = jnp.exp(log_gamma[:, None] * (idx + 1.0))
    decay_state = jnp.exp(log_gamma[:, None] * (C - 1.0 - idx))
    decay_chunk = jnp.exp(log_gamma * C)
    kz = k * decay_state[None, :, None, :, None]

    def step(r_st, inp):
        k_c, v_c = inp
        r_new = decay_chunk[None, :, None, None] * r_st + jnp.einsum('bhjk,bhjv->bhkv', k_c, v_c)
        return r_new, r_st

    _, r_prev = lax.scan(step, jnp.zeros((B, H, dk, dv), F32), (jnp.moveaxis(kz, 2, 0), jnp.moveaxis(v, 2, 0)))
    r_prev = jnp.moveaxis(r_prev, 0, 2)
    s = jnp.einsum('bhncd,bhnjd->bhncj', q, k) * decay_intra[None, :, None]
    inner = jnp.einsum('bhncj,bhnjv->bhncv', s, v)
    cross = decay_cross[None, :, None, :, None] * jnp.einsum('bhncd,bhndv->bhncv', q, r_prev)
    return inner + cross


def mixer_sublayer(h, norm_w, w_in, gate_bias, m_norm_w, r_norm_w, w_out):
    B, L, _ = h.shape
    a = rmsnorm(h, norm_w)
    a = jnp.pad(a, ((0, 0), (PAD, 0), (0, 0)))
    Lp = L + PAD
    n_chunks = Lp // CHUNK
    proj = (a @ w_in).astype(F32)
    sizes = [M_WIDTH] * 4 + [R_WIDTH] * 4
    offsets = [int(o) for o in np.cumsum(sizes)]
    mq, mk, mv, mo, rq, rk, rv, rg, gates = jnp.split(proj, offsets, axis=-1)
    gates = gates + gate_bias.astype(F32)
    gates = gates.transpose(0, 2, 1).reshape(B, 2 * M_HEADS, n_chunks, CHUNK)
    i_pre, f_pre = gates[:, :M_HEADS], gates[:, M_HEADS:]

    h_m = mlstm_heads(to_chunks(split_heads(mq, M_HEADS)), to_chunks(split_heads(mk, M_HEADS)),
                      to_chunks(split_heads(mv, M_HEADS)), i_pre, f_pre)
    h_m = h_m.reshape(B, M_HEADS, Lp, HEAD_DIM)
    out_m = merge_heads(head_norm(jax.nn.sigmoid(split_heads(mo, M_HEADS)) * h_m, m_norm_w))

    pos = jnp.arange(Lp, dtype=F32) - PAD
    rq_h = rotary(split_heads(rq, R_HEADS), pos)
    rk_h = rotary(split_heads(rk, R_HEADS), pos)
    ret = retention_heads(to_chunks(rq_h), to_chunks(rk_h), to_chunks(split_heads(rv, R_HEADS)))
    ret = ret.reshape(B, R_HEADS, Lp, HEAD_DIM)
    out_r = merge_heads(jax.nn.silu(split_heads(rg, R_HEADS)) * head_norm(ret, r_norm_w))

    y = jnp.concatenate([out_m, out_r], axis=-1)[:, PAD:].astype(h.dtype)
    return y @ w_out


def peer_ffn(x, w_query, sub_keys, w_down, w_up):
    B, L, D = x.shape
    T = B * L
    TB = PEER_TOKEN_BLOCK
    n_blk = -(-T // TB)
    t = jnp.pad(x.reshape(T, D), ((0, n_blk * TB - T), (0, 0))).reshape(n_blk, TB, D)

    def block(xb):
        q = (xb @ w_query).astype(F32).reshape(TB, PEER_HEADS, 2, PEER_QDIM // 2)
        s = jnp.einsum('thpd,pnd->thpn', q, sub_keys.astype(F32))
        s1, i1 = lax.top_k(s[:, :, 0], PEER_TOPK)
        s2, i2 = lax.top_k(s[:, :, 1], PEER_TOPK)
        cand = (s1[..., :, None] + s2[..., None, :]).reshape(TB, PEER_HEADS, PEER_TOPK * PEER_TOPK)
        cidx = (i1[..., :, None] * PEER_KEYS + i2[..., None, :]).reshape(TB, PEER_HEADS, PEER_TOPK * PEER_TOPK)
        top, sel = lax.top_k(cand, PEER_TOPK)
        eidx = jnp.take_along_axis(cidx, sel, axis=-1)
        gate = jax.nn.softmax(top, axis=-1)
        u = jnp.take(w_down, eidx, axis=0)
        act = jax.nn.gelu(jnp.einsum('thkd,td->thk', u, xb).astype(F32), approximate=False)
        vv = jnp.take(w_up, eidx, axis=0)
        return jnp.einsum('thk,thkd->td', (gate * act).astype(xb.dtype), vv)

    out = lax.map(block, t).reshape(n_blk * TB, D)[:T]
    return out.reshape(B, L, D)


def setup_inputs(seed: int = 0) -> dict:
    key = jax.random.key(seed)
    ks = jax.random.split(key, 16)
    nrm = jax.random.normal
    x = nrm(ks[0], (BATCH, SEQ, D_MODEL), F32)
    meta_tokens = nrm(ks[1], (N_META, D_MODEL), F32)
    norm1_w = 1.0 + 0.01 * nrm(ks[2], (DEPTH, D_MODEL), F32)
    w_in = nrm(ks[3], (DEPTH, D_MODEL, PROJ_WIDTH), F32) * D_MODEL ** -0.5
    i_bias = 0.1 * nrm(ks[4], (DEPTH, M_HEADS), F32)
    f_bias = jnp.linspace(3.0, 6.0, M_HEADS, dtype=F32)[None, :] + 0.01 * nrm(ks[5], (DEPTH, M_HEADS), F32)
    gate_bias = jnp.concatenate([i_bias, f_bias], axis=-1)
    m_norm_w = 1.0 + 0.01 * nrm(ks[6], (DEPTH, M_WIDTH), F32)
    r_norm_w = 1.0 + 0.01 * nrm(ks[7], (DEPTH, R_WIDTH), F32)
    w_out = nrm(ks[8], (DEPTH, MIX_WIDTH, D_MODEL), F32) * MIX_WIDTH ** -0.5
    norm2_w = 1.0 + 0.01 * nrm(ks[9], (DEPTH, D_MODEL), F32)
    peer_w_query = nrm(ks[10], (DEPTH, D_MODEL, PEER_HEADS * PEER_QDIM), F32) * D_MODEL ** -0.5
    peer_sub_keys = nrm(ks[11], (DEPTH, 2, PEER_KEYS, PEER_QDIM // 2), F32) * (PEER_QDIM // 2) ** -0.5
    peer_w_down = nrm(ks[12], (DEPTH, PEER_EXPERTS, D_MODEL), F32) * D_MODEL ** -0.5
    peer_w_up = nrm(ks[13], (DEPTH, PEER_EXPERTS, D_MODEL), F32) * PEER_HEADS ** -0.5
    final_norm_w = 1.0 + 0.01 * nrm(ks[14], (D_MODEL,), F32)
    return {"x": x, "meta_tokens": meta_tokens, "norm1_w": norm1_w, "w_in": w_in,
            "gate_bias": gate_bias, "m_norm_w": m_norm_w, "r_norm_w": r_norm_w, "w_out": w_out,
            "norm2_w": norm2_w, "peer_w_query": peer_w_query, "peer_sub_keys": peer_sub_keys,
            "peer_w_down": peer_w_down, "peer_w_up": peer_w_up, "final_norm_w": final_norm_w}


def reference(x, meta_tokens, norm1_w, w_in, gate_bias, m_norm_w, r_norm_w, w_out, norm2_w,
              peer_w_query, peer_sub_keys, peer_w_down, peer_w_up, final_norm_w):
    B = x.shape[0]
    meta = jnp.broadcast_to(meta_tokens.astype(x.dtype)[None], (B, N_META, D_MODEL))
    h = jnp.concatenate([meta, x], axis=1)
    for l in range(DEPTH):
        h = h + mixer_sublayer(h, norm1_w[l], w_in[l], gate_bias[l], m_norm_w[l], r_norm_w[l], w_out[l])
        h = h + peer_ffn(rmsnorm(h, norm2_w[l]), peer_w_query[l], peer_sub_keys[l], peer_w_down[l], peer_w_up[l])
    return rmsnorm(h, final_norm_w)[:, N_META:]
```

```python
import functools

import jax
import jax.numpy as jnp
from jax import lax
from jax.experimental import pallas as pl
from jax.experimental.pallas import tpu as pltpu

F32 = jnp.float32
BF16 = jnp.bfloat16
I32 = jnp.int32

D_MODEL = 1024
N_META = 16
CHUNK = 64
PAD = CHUNK - N_META
M_HEADS = 4
R_HEADS = 4
HEAD_DIM = D_MODEL // (M_HEADS + R_HEADS)
M_WIDTH = M_HEADS * HEAD_DIM
R_WIDTH = R_HEADS * HEAD_DIM
PROJ_MAIN = 4 * M_WIDTH + 4 * R_WIDTH
N_GATES = 2 * M_HEADS
PEER_HEADS = 8
PEER_KEYS = 128
PEER_EXPERTS = PEER_KEYS * PEER_KEYS
PEER_TOPK = 16
PEER_QDIM = 256
ROPE_BASE = 10000.0
EPS = 1e-6

LANES = 128
SUBLANES = 8
VMEM_LIMIT_BYTES = 56 * 1024 * 1024

NEG_INF = float("-inf")
HIGHEST = lax.Precision.HIGHEST

NN = ((1,), (0,))
NT = ((1,), (1,))
TN = ((0,), (0,))


def _dot(a, b, dims, precision=None):
    return lax.dot_general(a, b, (dims, ((), ())), preferred_element_type=F32, precision=precision)


def _rms(x, w):
    return x * lax.rsqrt(jnp.mean(x * x, axis=-1, keepdims=True) + EPS) * w


def _gelu(x):
    return 0.5 * x * (1.0 + lax.erf(x * (2.0 ** -0.5)))


def _head_norm(x, w):
    mu = jnp.mean(x, axis=-1, keepdims=True)
    xc = x - mu
    var = jnp.mean(xc * xc, axis=-1, keepdims=True)
    return xc * lax.rsqrt(var + EPS) * w


def _mixer_kernel(x_ref, pre_ref, n1w_ref, wmain_ref, wg_ref, wgt_ref, gbc_ref, gbr_ref,
                  mnw_ref, rnw_ref, wout_ref, csx_ref, snx_ref, csp_ref, snp_ref,
                  ltri_ref, ltrit_ref, dintra_ref, dcross_ref, dstate_ref, dchunk_ref,
                  out_ref,
                  proj_scr, gcol_scr, grow_scr, y_scr, c_scr, n_scr, m_scr, r_scr,
                  *, tile_rows):
    j = pl.program_id(1)
    n_chunks = tile_rows // CHUNK
    kscale = HEAD_DIM ** -0.5

    def project(xt, rows, cs, sn):
        a = _rms(xt, n1w_ref[...])
        ab = a.astype(BF16)
        for seg in range(PROJ_MAIN // M_WIDTH):
            cols = slice(seg * M_WIDTH, (seg + 1) * M_WIDTH)
            p = _dot(ab, wmain_ref[:, cols], NN)
            if seg in (4, 5):
                for h in range(R_HEADS):
                    ph = p[:, h * HEAD_DIM:(h + 1) * HEAD_DIM]
                    ph = ph * cs + pltpu.roll(ph, HEAD_DIM // 2, 1) * sn
                    proj_scr[0:rows, seg * M_WIDTH + h * HEAD_DIM: seg * M_WIDTH + (h + 1) * HEAD_DIM] = ph
            else:
                proj_scr[0:rows, cols] = p
        gcol_scr[0:rows, :] = _dot(a, wg_ref[...], NN, HIGHEST) + gbc_ref[...]
        grow = _dot(wgt_ref[...], a, NT, HIGHEST) + gbr_ref[...]
        for c in range(rows // CHUNK):
            grow_scr[c] = grow[:, c * CHUNK:(c + 1) * CHUNK]

    def chunk(c, want_out):
        r0 = pl.multiple_of(c * CHUNK, CHUNK)
        rows = pl.ds(r0, CHUNK)

        def col(seg, h):
            return proj_scr[rows, seg * M_WIDTH + h * HEAD_DIM: seg * M_WIDTH + (h + 1) * HEAD_DIM]

        ltri = ltri_ref[...]
        causal = ltri > 0.5
        gc = gcol_scr[rows, :]
        logf_c = jax.nn.log_sigmoid(gc)
        bc_all = _dot(ltri, logf_c, NN, HIGHEST)
        gr = grow_scr[c]
        logf_r = jax.nn.log_sigmoid(gr[M_HEADS:, :])
        br = _dot(logf_r, ltrit_ref[...], NN, HIGHEST)
        ir = gr[:M_HEADS, :]

        for h in range(M_HEADS):
            q = col(0, h)
            k = col(1, h) * kscale
            v = col(2, h)
            qb, kb, vb = q.astype(BF16), k.astype(BF16), v.astype(BF16)
            b_c = bc_all[:, M_HEADS + h:M_HEADS + h + 1]
            i_c = gc[:, h:h + 1]
            m_prev = m_scr[h, 0:1, 0:1]
            c_prev = c_scr[h]
            n_prev = n_scr[h]
            if want_out:
                rowv = br[h:h + 1, :] - ir[h:h + 1, :]
                d_log = jnp.where(causal, b_c - rowv, NEG_INF)
                inter_log = b_c + m_prev
                m_t = jnp.maximum(inter_log, jnp.max(d_log, axis=1, keepdims=True))
                w_inter = jnp.exp(inter_log - m_t)
                s = _dot(qb, kb, NT) * jnp.exp(d_log - m_t)
                num = w_inter * _dot(qb, c_prev.astype(BF16), NN) + _dot(s.astype(BF16), vb, NN)
                den = (w_inter * jnp.sum(q * n_prev, axis=1, keepdims=True)
                       + jnp.sum(s, axis=1, keepdims=True))
                h_tilde = num / jnp.maximum(jnp.abs(den), jnp.exp(-m_t))
                o = col(3, h)
                om = _head_norm(jax.nn.sigmoid(o) * h_tilde, mnw_ref[:, h * HEAD_DIM:(h + 1) * HEAD_DIM])
                y_scr[rows, h * HEAD_DIM:(h + 1) * HEAD_DIM] = om.astype(BF16)
            g = bc_all[CHUNK - 1:CHUNK, M_HEADS + h:M_HEADS + h + 1]
            a_c = g - b_c + i_c
            m_new = jnp.maximum(g + m_prev, jnp.max(a_c, axis=0, keepdims=True))
            decay = jnp.exp(g + m_prev - m_new)
            kw = k * jnp.exp(a_c - m_new)
            c_scr[h] = decay * c_prev + _dot(kw.astype(BF16), vb, TN)
            n_scr[h] = decay * n_prev + jnp.sum(kw, axis=0, keepdims=True)
            m_scr[h] = jnp.broadcast_to(m_new, (SUBLANES, LANES))

        for h in range(R_HEADS):
            q = col(4, h)
            k = col(5, h) * kscale
            v = col(6, h)
            qb, kb, vb = q.astype(BF16), k.astype(BF16), v.astype(BF16)
            r_prev = r_scr[h]
            if want_out:
                s = _dot(qb, kb, NT) * dintra_ref[h]
                ret = _dot(s.astype(BF16), vb, NN) + dcross_ref[h] * _dot(qb, r_prev.astype(BF16), NN)
                gate = col(7, h)
                orr = jax.nn.silu(gate) * _head_norm(ret, rnw_ref[:, h * HEAD_DIM:(h + 1) * HEAD_DIM])
                y_scr[rows, M_WIDTH + h * HEAD_DIM: M_WIDTH + (h + 1) * HEAD_DIM] = orr.astype(BF16)
            kz = (k * dstate_ref[h]).astype(BF16)
            r_scr[h] = dchunk_ref[h] * r_prev + _dot(kz, vb, TN)

    @pl.when(j == 0)
    def _():
        c_scr[...] = jnp.zeros_like(c_scr)
        n_scr[...] = jnp.zeros_like(n_scr)
        m_scr[...] = jnp.zeros_like(m_scr)
        r_scr[...] = jnp.zeros_like(r_scr)
        project(pre_ref[...], CHUNK, csp_ref[...], snp_ref[...])
        chunk(0, False)

    x = x_ref[0]
    project(x, tile_rows, csx_ref[...], snx_ref[...])

    def body(c, carry):
        chunk(c, True)
        return carry

    lax.fori_loop(0, n_chunks, body, 0)
    out_ref[0] = x + _dot(y_scr[...], wout_ref[...], NN)


def _mixer(x, prefix, n1w, wmain, wg, wgt, gbc, gbr, mnw, rnw, wout, cs, sn, consts, tile_rows):
    B, L, D = x.shape
    nt = L // tile_rows
    ltri, ltrit, dintra, dcross, dstate, dchunk = consts
    csp, csx = cs[:CHUNK], cs[CHUNK:]
    snp, snx = sn[:CHUNK], sn[CHUNK:]

    def full(a):
        return pl.BlockSpec(a.shape, lambda b, j, _n=a.ndim: (0,) * _n)

    in_specs = [
        pl.BlockSpec((1, tile_rows, D), lambda b, j: (b, j, 0)),
        full(prefix), full(n1w), full(wmain), full(wg), full(wgt), full(gbc), full(gbr),
        full(mnw), full(rnw), full(wout),
        pl.BlockSpec((tile_rows, HEAD_DIM), lambda b, j: (j, 0)),
        pl.BlockSpec((tile_rows, HEAD_DIM), lambda b, j: (j, 0)),
        full(csp), full(snp), full(ltri), full(ltrit), full(dintra), full(dcross), full(dstate), full(dchunk),
    ]
    scratch = [
        pltpu.VMEM((tile_rows, PROJ_MAIN), F32),
        pltpu.VMEM((tile_rows, LANES), F32),
        pltpu.VMEM((tile_rows // CHUNK, N_GATES, CHUNK), F32),
        pltpu.VMEM((tile_rows, D), BF16),
        pltpu.VMEM((M_HEADS, HEAD_DIM, HEAD_DIM), F32),
        pltpu.VMEM((M_HEADS, 1, HEAD_DIM), F32),
        pltpu.VMEM((M_HEADS, SUBLANES, LANES), F32),
        pltpu.VMEM((R_HEADS, HEAD_DIM, HEAD_DIM), F32),
    ]
    return pl.pallas_call(
        functools.partial(_mixer_kernel, tile_rows=tile_rows),
        grid=(B, nt),
        in_specs=in_specs,
        out_specs=pl.BlockSpec((1, tile_rows, D), lambda b, j: (b, j, 0)),
        out_shape=jax.ShapeDtypeStruct((B, L, D), F32),
        scratch_shapes=scratch,
        compiler_params=pltpu.CompilerParams(
            dimension_semantics=("arbitrary", "arbitrary"),
            vmem_limit_bytes=VMEM_LIMIT_BYTES),
        name="mixer",
    )(x, prefix, n1w, wmain, wg, wgt, gbc, gbr, mnw, rnw, wout, csx, snx, csp, snp,
      ltri, ltrit, dintra, dcross, dstate, dchunk)


def _extract_top(v, ids, none_id):
    rank = jnp.full(v.shape, PEER_TOPK, I32)
    row16 = lax.broadcasted_iota(I32, (PEER_TOPK, LANES), 0)
    top = jnp.zeros((PEER_TOPK, LANES), F32)
    for i in range(PEER_TOPK):
        m = jnp.max(v, axis=0, keepdims=True)
        first = jnp.min(jnp.where(v == m, ids, none_id), axis=0, keepdims=True)
        hit = ids == first
        rank = jnp.where(hit, i, rank)
        v = jnp.where(hit, NEG_INF, v)
        top = jnp.where(row16 == i, m, top)
    return rank, top


def _select_pairs(t1, t2):
    row8 = lax.broadcasted_iota(I32, (SUBLANES, LANES), 0)
    groups, ids = [], []
    groups.append(t1[0:1] + t2)
    ids.append(lax.broadcasted_iota(I32, (PEER_TOPK, LANES), 0))
    for i in range(1, SUBLANES):
        n_i = PEER_TOPK // (i + 1)
        groups.append(jnp.where(row8 < n_i, t1[i:i + 1] + t2[0:SUBLANES], NEG_INF))
        ids.append(row8 + i * PEER_TOPK)
    groups.append(t1[SUBLANES:] + t2[0:1])
    ids.append((row8 + SUBLANES) * PEER_TOPK)
    cand = jnp.concatenate(groups, axis=0)
    cid = jnp.concatenate(ids, axis=0)
    rank, _ = _extract_top(cand, cid, PEER_TOPK * PEER_TOPK)
    sel = rank < PEER_TOPK
    top = t1[0:1] + t2[0:1]
    z = jnp.sum(jnp.where(sel, jnp.exp(cand - top), 0.0), axis=0, keepdims=True)
    self32 = sel.astype(F32)
    c = jnp.zeros((PEER_TOPK, LANES), F32)
    row16 = lax.broadcasted_iota(I32, (PEER_TOPK, LANES), 0)
    c = jnp.where(row16 == 0, jnp.sum(self32[0:PEER_TOPK], axis=0, keepdims=True), c)
    for i in range(1, SUBLANES):
        lo = PEER_TOPK + (i - 1) * SUBLANES
        c = jnp.where(row16 == i, jnp.sum(self32[lo:lo + SUBLANES], axis=0, keepdims=True), c)
    tail = self32[PEER_TOPK + (SUBLANES - 1) * SUBLANES:]
    c = jnp.where(row16 >= SUBLANES, jnp.concatenate([tail, tail], axis=0), c)
    return c, z


def _peer_kernel(h_ref, n2w_ref, wqt_ref, keys_ref, wd_ref, wut_ref, fnw_ref, out_ref,
                 xn_scr, s_scr, e1_scr, cnt_scr, e2_scr, r2_scr, a_scr, ga_scr, outt_scr,
                 *, tile_tokens, block_experts):
    eb = pl.program_id(1)
    n_eb = pl.num_programs(1)
    nlt = tile_tokens // LANES
    a_per_block = block_experts // PEER_KEYS
    half = PEER_QDIM // 2

    @pl.when(eb == 0)
    def _():
        xn = _rms(h_ref[...], n2w_ref[...]).astype(BF16)
        xn_scr[...] = xn
        for h in range(PEER_HEADS):
            qt = _dot(wqt_ref[h * PEER_QDIM:(h + 1) * PEER_QDIM, :], xn, NT)
            for p in range(2):
                s = _dot(keys_ref[p], qt[p * half:(p + 1) * half].astype(BF16), NN)
                for lt in range(nlt):
                    s_scr[h, p, lt] = s[:, lt * LANES:(lt + 1) * LANES]

        key_ids = lax.broadcasted_iota(I32, (PEER_KEYS, LANES), 0)

        def select(idx, carry):
            h = idx // nlt
            lt = idx % nlt
            s1 = s_scr[h, 0, lt]
            s2 = s_scr[h, 1, lt]
            r1, t1 = _extract_top(s1, key_ids, PEER_KEYS)
            r2, t2 = _extract_top(s2, key_ids, PEER_KEYS)
            c, z = _select_pairs(t1, t2)
            cnt = jnp.zeros((PEER_KEYS, LANES), F32)
            for i in range(PEER_TOPK):
                cnt = jnp.where(r1 == i, c[i:i + 1], cnt)
            e1_scr[h, lt] = jnp.exp(s1 - t1[0:1]) / z
            cnt_scr[h, lt] = cnt
            e2_scr[h, lt] = jnp.exp(s2 - t2[0:1])
            r2_scr[h, lt] = r2.astype(F32)
            return carry

        lax.fori_loop(0, PEER_HEADS * nlt, select, 0)
        outt_scr[...] = jnp.zeros_like(outt_scr)

    a_scr[...] = _dot(wd_ref[...], xn_scr[...], NT)

    def gate_block(al, carry):
        a = eb * a_per_block + al
        rows = pl.ds(pl.multiple_of(al * PEER_KEYS, PEER_KEYS), PEER_KEYS)
        for lt in range(nlt):
            lanes = slice(lt * LANES, (lt + 1) * LANES)
            act = _gelu(a_scr[rows, lanes])
            g = jnp.zeros((PEER_KEYS, LANES), F32)
            for h in range(PEER_HEADS):
                cnt = cnt_scr[h, lt, pl.ds(a, 1), :]
                e1 = e1_scr[h, lt, pl.ds(a, 1), :]
                g = g + jnp.where(r2_scr[h, lt] < cnt, e1 * e2_scr[h, lt], 0.0)
            ga_scr[rows, lanes] = (g * act).astype(BF16)
        return carry

    lax.fori_loop(0, a_per_block, gate_block, 0)
    outt_scr[...] += _dot(wut_ref[...], ga_scr[...], NN)

    @pl.when(eb == n_eb - 1)
    def _():
        h2 = h_ref[...] + outt_scr[...].T
        out_ref[...] = _rms(h2, fnw_ref[...])


def _peer(h, n2w, wqt, keys, wd, wut, fnw, tile_tokens, block_experts):
    T, D = h.shape
    nlt = tile_tokens // LANES
    grid = (T // tile_tokens, PEER_EXPERTS // block_experts)

    def full(a):
        return pl.BlockSpec(a.shape, lambda i, e, _n=a.ndim: (0,) * _n)

    in_specs = [
        pl.BlockSpec((tile_tokens, D), lambda i, e: (i, 0)),
        full(n2w), full(wqt), full(keys),
        pl.BlockSpec((block_experts, D), lambda i, e: (e, 0)),
        pl.BlockSpec((D, block_experts), lambda i, e: (0, e)),
        full(fnw),
    ]
    dense = (PEER_HEADS, nlt, PEER_KEYS, LANES)
    scratch = [
        pltpu.VMEM((tile_tokens, D), BF16),
        pltpu.VMEM((PEER_HEADS, 2, nlt, PEER_KEYS, LANES), F32),
        pltpu.VMEM(dense, F32), pltpu.VMEM(dense, F32), pltpu.VMEM(dense, F32), pltpu.VMEM(dense, F32),
        pltpu.VMEM((block_experts, tile_tokens), F32),
        pltpu.VMEM((block_experts, tile_tokens), BF16),
        pltpu.VMEM((D, tile_tokens), F32),
    ]
    return pl.pallas_call(
        functools.partial(_peer_kernel, tile_tokens=tile_tokens, block_experts=block_experts),
        grid=grid,
        in_specs=in_specs,
        out_specs=pl.BlockSpec((tile_tokens, D), lambda i, e: (i, 0)),
        out_shape=jax.ShapeDtypeStruct((T, D), F32),
        scratch_shapes=scratch,
        compiler_params=pltpu.CompilerParams(
            dimension_semantics=("arbitrary", "arbitrary"),
            vmem_limit_bytes=VMEM_LIMIT_BYTES),
        name="peer",
    )(h, n2w, wqt, keys, wd, wut, fnw)


def _rotary_tables(n_rows):
    half = HEAD_DIM // 2
    pos = jnp.arange(n_rows, dtype=F32) - PAD
    inv_freq = ROPE_BASE ** (-jnp.arange(half, dtype=F32) / half)
    ang = pos[:, None] * inv_freq[None, :]
    cos, sin = jnp.cos(ang), jnp.sin(ang)
    return jnp.concatenate([cos, cos], axis=1), jnp.concatenate([-sin, sin], axis=1)


def _chunk_constants():
    C = CHUNK
    idx = jnp.arange(C, dtype=F32)
    rel = idx[:, None] - idx[None, :]
    causal = rel >= 0
    ltri = causal.astype(F32)
    log_gamma = jnp.log1p(-jnp.exp2(-5.0 - jnp.arange(R_HEADS, dtype=F32)))
    dintra = jnp.where(causal, jnp.exp(log_gamma[:, None, None] * jnp.where(causal, rel, 0.0)), 0.0)
    dcross = jnp.exp(log_gamma[:, None] * (idx + 1.0))[:, :, None]
    dstate = jnp.exp(log_gamma[:, None] * (C - 1.0 - idx))[:, :, None]
    dchunk = jnp.exp(log_gamma * C)[:, None, None]
    return ltri, ltri.T, dintra, dcross, dstate, dchunk


def _pick_tile(n, target):
    t = min(n, target)
    while n % t:
        t //= 2
    return t


def kernel(x, meta_tokens, norm1_w, w_in, gate_bias, m_norm_w, r_norm_w, w_out, norm2_w,
           peer_w_query, peer_sub_keys, peer_w_down, peer_w_up, final_norm_w):
    B, L, D = x.shape
    assert D == D_MODEL and norm1_w.shape[0] == 1 and L % CHUNK == 0
    mixer_rows = _pick_tile(L, 512)
    tile_tokens = _pick_tile(B * L, 512)
    block_experts = 1024

    w = w_in[0]
    wmain = w[:, :PROJ_MAIN].astype(BF16)
    wgate = w[:, PROJ_MAIN:]
    wg = jnp.pad(wgate, ((0, 0), (0, LANES - N_GATES)))
    gbc = jnp.pad(gate_bias[0], (0, LANES - N_GATES))[None, :]
    gbr = gate_bias[0][:, None]
    prefix = jnp.concatenate([jnp.zeros((PAD, D), x.dtype), meta_tokens.astype(x.dtype)], axis=0)
    cs, sn = _rotary_tables(L + CHUNK)

    h1 = _mixer(x, prefix, norm1_w[0][None, :], wmain, wg, wgate.T, gbc, gbr,
                m_norm_w[0][None, :], r_norm_w[0][None, :], w_out[0].astype(BF16),
                cs, sn, _chunk_constants(), mixer_rows)

    out = _peer(h1.reshape(B * L, D), norm2_w[0][None, :],
                peer_w_query[0].T.astype(BF16), peer_sub_keys[0].astype(BF16),
                peer_w_down[0].astype(BF16), peer_w_up[0].T.astype(BF16),
                final_norm_w[None, :], tile_tokens, block_experts)
    return out.reshape(B, L, D)
```
